```python
import math
import jax, jax.numpy as jnp
from jax import lax
import numpy as np

D_MODEL = 2048
BATCH = 4
SEQ = 2048
DEPTH = 4

GRID_W = 64
CTX_LEN = 256
ATTN_HEADS = 8
ATTN_KV_HEADS = 2
GQA_GROUP = ATTN_HEADS // ATTN_KV_HEADS
HEAD_DIM = 128
ATTN_WIDTH = ATTN_HEADS * HEAD_DIM
KV_WIDTH = ATTN_KV_HEADS * HEAD_DIM
Q_BLOCK = 128
ROPE_THETA = 10000.0
ROPE_PAIRS = HEAD_DIM // 4
SSD_INNER = D_MODEL // 2
SSD_HEAD_DIM = 64
SSD_HEADS = SSD_INNER // SSD_HEAD_DIM
SSD_GROUPS = 2
SSD_HEADS_PER_GROUP = SSD_HEADS // SSD_GROUPS
SSD_STATE = 128
SSD_CONV = 3
SSD_CHUNK = 128
XBC_WIDTH = SSD_INNER + 2 * SSD_GROUPS * SSD_STATE
MIX_WIDTH = ATTN_WIDTH + SSD_INNER
IN_COLS = ATTN_WIDTH + 2 * KV_WIDTH + SSD_INNER + XBC_WIDTH + 2 * SSD_HEADS
D_FF = 5632
FFN_CONV = 3
N_MOD = 6
EPS = 1e-6

kernel_name = "hymba_gqa_ssd_convffn_prefix_dit"


def rms_norm(x, w):
    xf = x.astype(jnp.float32)
    y = xf * lax.rsqrt(jnp.mean(xf * xf, axis=-1, keepdims=True) + EPS)
    return (y * w.astype(jnp.float32)).astype(x.dtype)


def depthwise_conv(x, w, b):
    k = w.shape[0]
    y = lax.conv_general_dilated(x, w[:, None, :], window_strides=(1,),
                                 padding=((k // 2, k // 2),),
                                 dimension_numbers=('NWC', 'WIO', 'NWC'),
                                 feature_group_count=x.shape[-1])
    return y + b


def axial_rope_tables(seq_len):
    rows = seq_len // GRID_W
    row, col = jnp.meshgrid(jnp.arange(rows), jnp.arange(GRID_W), indexing='ij')
    inv_freq = ROPE_THETA ** (-jnp.arange(ROPE_PAIRS, dtype=jnp.float32) / ROPE_PAIRS)
    ang_r = row.reshape(-1).astype(jnp.float32)[:, None] * inv_freq
    ang_c = col.reshape(-1).astype(jnp.float32)[:, None] * inv_freq
    return (jnp.cos(ang_r), jnp.sin(ang_r), jnp.cos(ang_c), jnp.sin(ang_c))


def rope_half(x, cos, sin):
    x1, x2 = jnp.split(x, 2, axis=-1)
    c = cos[:, None, :]
    s = sin[:, None, :]
    return jnp.concatenate([x1 * c - x2 * s, x2 * c + x1 * s], axis=-1)


def apply_axial_rope(x, tables):
    cos_r, sin_r, cos_c, sin_c = tables
    x_row, x_col = jnp.split(x, 2, axis=-1)
    return jnp.concatenate([rope_half(x_row, cos_r, sin_r), rope_half(x_col, cos_c, sin_c)], axis=-1).astype(x.dtype)


def modulation(cond, w_ada, b_ada):
    return (jax.nn.silu(cond) @ w_ada + b_ada)[:, None, :]


def split_projection(p):
    bsz, t = p.shape[:2]
    cuts = [ATTN_WIDTH, ATTN_WIDTH + KV_WIDTH, ATTN_WIDTH + 2 * KV_WIDTH,
            ATTN_WIDTH + 2 * KV_WIDTH + SSD_INNER,
            ATTN_WIDTH + 2 * KV_WIDTH + SSD_INNER + XBC_WIDTH]
    q, k, v, z, xbc, dt = jnp.split(p, cuts, axis=-1)
    q = q.reshape(bsz, t, ATTN_HEADS, HEAD_DIM)
    k = k.reshape(bsz, t, ATTN_KV_HEADS, HEAD_DIM)
    v = v.reshape(bsz, t, ATTN_KV_HEADS, HEAD_DIM)
    return q, k, v, z, xbc, dt


def grouped_sdpa(q, k, v):
    s = jnp.einsum('bqkgd,bskd->bkgqs', q, k).astype(jnp.float32) * (HEAD_DIM ** -0.5)
    p = jax.nn.softmax(s, axis=-1).astype(v.dtype)
    return jnp.einsum('bkgqs,bskd->bqkgd', p, v)


def attention_mixer(q_l, k_l, v_l, q_c, k_c, v_c, rope, q_norm_w, k_norm_w, need_ctx_out):
    bsz, seq_len = q_l.shape[:2]
    ctx_len = q_c.shape[1]
    q_l = apply_axial_rope(rms_norm(q_l, q_norm_w), rope)
    k_l = apply_axial_rope(rms_norm(k_l, k_norm_w), rope)
    q_c = rms_norm(q_c, q_norm_w)
    k_c = rms_norm(k_c, k_norm_w)
    k_all = jnp.concatenate([k_c, k_l], axis=1)
    v_all = jnp.concatenate([v_c, v_l], axis=1)
    n_blk = seq_len // Q_BLOCK
    q_blocks = q_l.reshape(bsz, n_blk, Q_BLOCK, ATTN_KV_HEADS, GQA_GROUP, HEAD_DIM).swapaxes(0, 1)
    o_l = lax.map(lambda qb: grouped_sdpa(qb, k_all, v_all), q_blocks)
    o_l = o_l.swapaxes(0, 1).reshape(bsz, seq_len, ATTN_WIDTH)
    o_c = None
    if need_ctx_out:
        q_cg = q_c.reshape(bsz, ctx_len, ATTN_KV_HEADS, GQA_GROUP, HEAD_DIM)
        o_c = grouped_sdpa(q_cg, k_c, v_c).reshape(bsz, ctx_len, ATTN_WIDTH)
    return o_l, o_c


def ssd_chunked(xs, dt, a, bm, cm, init_state):
    bsz, t = xs.shape[:2]
    nc = t // SSD_CHUNK
    e = SSD_HEADS_PER_GROUP
    xdt = (xs.astype(jnp.float32) * dt[..., None]).reshape(bsz, nc, SSD_CHUNK, SSD_GROUPS, e, SSD_HEAD_DIM)
    da = (dt * a).reshape(bsz, nc, SSD_CHUNK, SSD_GROUPS, e)
    bm = bm.astype(jnp.float32).reshape(bsz, nc, SSD_CHUNK, SSD_GROUPS, SSD_STATE)
    cm = cm.astype(jnp.float32).reshape(bsz, nc, SSD_CHUNK, SSD_GROUPS, SSD_STATE)
    cs = jnp.cumsum(da, axis=2)
    causal = jnp.tril(jnp.ones((SSD_CHUNK, SSD_CHUNK), dtype=bool))[:, :, None, None]
    seg = cs[:, :, :, None] - cs[:, :, None, :]
    decay = jnp.exp(jnp.where(causal, seg, -jnp.inf))
    cb = jnp.einsum('bclgn,bcsgn->bclsg', cm, bm)
    y_diag = jnp.einsum('bclsg,bclsge,bcsgep->bclgep', cb, decay, xdt)
    to_end = jnp.exp(cs[:, :, -1:] - cs)
    chunk_states = jnp.einsum('bclgn,bclge,bclgep->bcgepn', bm, to_end, xdt)
    chunk_decay = jnp.exp(cs[:, :, -1])

    def carry_step(state, inp):
        st, dc = inp
        return state * dc[..., None, None] + st, state

    final_state, prev = lax.scan(carry_step, init_state,
                                 (jnp.moveaxis(chunk_states, 1, 0), jnp.moveaxis(chunk_decay, 1, 0)))
    prev = jnp.moveaxis(prev, 0, 1)
    y_off = jnp.einsum('bclgn,bcgepn,bclge->bclgep', cm, prev, jnp.exp(cs))
    y = (y_diag + y_off).reshape(bsz, t, SSD_HEADS, SSD_HEAD_DIM)
    return y, final_state


def ssd_mixer(z_l, xbc_l, dtr_l, z_c, xbc_c, dtr_c, conv_w, conv_b, dt_bias, a_log, d_skip, norm_w, need_ctx_out):
    a = -jnp.exp(a_log.astype(jnp.float32))

    def prep(xbc, dt_raw):
        xbc = jax.nn.silu(depthwise_conv(xbc, conv_w, conv_b))
        xs, bm, cm = jnp.split(xbc, [SSD_INNER, SSD_INNER + SSD_GROUPS * SSD_STATE], axis=-1)
        bsz, t = xs.shape[:2]
        xs = xs.reshape(bsz, t, SSD_HEADS, SSD_HEAD_DIM)
        bm = bm.reshape(bsz, t, SSD_GROUPS, SSD_STATE)
        cm = cm.reshape(bsz, t, SSD_GROUPS, SSD_STATE)
        dt = jax.nn.softplus(dt_raw.astype(jnp.float32).reshape(bsz, t, 2, SSD_HEADS) + dt_bias.astype(jnp.float32))
        return xs, bm, cm, dt

    xs_c, b_c, c_c, dt_c = prep(xbc_c, dtr_c)
    xs_l, b_l, c_l, dt_l = prep(xbc_l, dtr_l)
    bsz = xs_l.shape[0]
    flip = lambda u: jnp.flip(u, axis=1)
    zero = jnp.zeros((bsz, SSD_GROUPS, SSD_HEADS_PER_GROUP, SSD_HEAD_DIM, SSD_STATE), jnp.float32)
    y_cf, s_f = ssd_chunked(xs_c, dt_c[:, :, 0], a[0], b_c, c_c, zero)
    y_lf, _ = ssd_chunked(xs_l, dt_l[:, :, 0], a[0], b_l, c_l, s_f)
    y_cb, s_b = ssd_chunked(flip(xs_c), flip(dt_c[:, :, 1]), a[1], flip(b_c), flip(c_c), zero)
    y_lb, _ = ssd_chunked(flip(xs_l), flip(dt_l[:, :, 1]), a[1], flip(b_l), flip(c_l), s_b)

    def finish(y_f, y_b, xs, z):
        bsz_, t = xs.shape[:2]
        y = y_f + flip(y_b) + d_skip.astype(jnp.float32)[:, None] * xs.astype(jnp.float32)
        y = y.reshape(bsz_, t, SSD_INNER) * jax.nn.silu(z.astype(jnp.float32))
        return rms_norm(y, norm_w).astype(z.dtype)

    o_l = finish(y_lf, y_lb, xs_l, z_l)
    o_c = finish(y_cf, y_cb, xs_c, z_c) if need_ctx_out else None
    return o_l, o_c


def conv_ffn(h, w_up, conv_w, conv_b, w_down):
    u = depthwise_conv(h @ w_up, conv_w, conv_b)
    gate, val = jnp.split(u, 2, axis=-1)
    return (jax.nn.silu(gate) * val) @ w_down


def setup_inputs(seed: int = 0) -> dict:
    key = jax.random.key(seed)
    ks = jax.random.split(key, 24)
    f32 = jnp.float32

    def nrm(k, shape, scale):
        return jax.random.normal(k, shape, f32) * scale

    x = nrm(ks[0], (BATCH, SEQ, D_MODEL), 1.0)
    c = nrm(ks[1], (BATCH, D_MODEL), 1.0)
    ctx = nrm(ks[2], (BATCH, CTX_LEN, D_MODEL), 1.0)
    c_ctx = nrm(ks[3], (D_MODEL,), 1.0)
    w_ada = nrm(ks[4], (DEPTH, D_MODEL, N_MOD * D_MODEL), 0.5 * D_MODEL ** -0.5)
    b_ada = nrm(ks[5], (DEPTH, N_MOD * D_MODEL), 0.01)
    norm_mix_w = 1.0 + nrm(ks[6], (DEPTH, D_MODEL), 0.02)
    w_in_main = nrm(ks[7], (DEPTH, D_MODEL, IN_COLS - 2 * SSD_HEADS), D_MODEL ** -0.5)
    w_in_dt = nrm(ks[8], (DEPTH, D_MODEL, 2 * SSD_HEADS), 0.1 * D_MODEL ** -0.5)
    w_in = jnp.concatenate([w_in_main, w_in_dt], axis=-1)
    q_norm_w = 1.0 + nrm(ks[9], (DEPTH, HEAD_DIM), 0.02)
    k_norm_w = 1.0 + nrm(ks[10], (DEPTH, HEAD_DIM), 0.02)
    ssd_conv_w = nrm(ks[11], (DEPTH, SSD_CONV, XBC_WIDTH), SSD_CONV ** -0.5)
    ssd_conv_b = nrm(ks[12], (DEPTH, XBC_WIDTH), 0.01)
    dt0 = jnp.exp(jax.random.uniform(ks[13], (DEPTH, 2, SSD_HEADS), f32, math.log(1e-3), math.log(1e-1)))
    dt_bias = dt0 + jnp.log(-jnp.expm1(-dt0))
    a_log = jnp.log(jax.random.uniform(ks[14], (DEPTH, 2, SSD_HEADS), f32, 1.0, 16.0))
    d_skip = 1.0 + nrm(ks[15], (DEPTH, SSD_HEADS), 0.02)
    ssd_norm_w = 1.0 + nrm(ks[16], (DEPTH, SSD_INNER), 0.02)
    w_out = nrm(ks[17], (DEPTH, MIX_WIDTH, D_MODEL), MIX_WIDTH ** -0.5)
    norm_mlp_w = 1.0 + nrm(ks[18], (DEPTH, D_MODEL), 0.02)
    w_up = nrm(ks[19], (DEPTH, D_MODEL, 2 * D_FF), D_MODEL ** -0.5)
    ffn_conv_w = nrm(ks[20], (DEPTH, FFN_CONV, 2 * D_FF), FFN_CONV ** -0.5)
    ffn_conv_b = nrm(ks[21], (DEPTH, 2 * D_FF), 0.01)
    w_down = nrm(ks[22], (DEPTH, D_FF, D_MODEL), D_FF ** -0.5)
    return {"x": x, "c": c, "ctx": ctx, "c_ctx": c_ctx,
            "w_ada": w_ada, "b_ada": b_ada, "norm_mix_w": norm_mix_w, "w_in": w_in,
            "q_norm_w": q_norm_w, "k_norm_w": k_norm_w,
            "ssd_conv_w": ssd_conv_w, "ssd_conv_b": ssd_conv_b, "dt_bias": dt_bias, "a_log": a_log,
            "d_skip": d_skip, "ssd_norm_w": ssd_norm_w, "w_out": w_out, "norm_mlp_w": norm_mlp_w,
            "w_up": w_up, "ffn_conv_w": ffn_conv_w, "ffn_conv_b": ffn_conv_b, "w_down": w_down}


def reference(x, c, ctx, c_ctx, w_ada, b_ada, norm_mix_w, w_in, q_norm_w, k_norm_w,
              ssd_conv_w, ssd_conv_b, dt_bias, a_log, d_skip, ssd_norm_w, w_out, norm_mlp_w,
              w_up, ffn_conv_w, ffn_conv_b, w_down):
    seq_len = x.shape[1]
    rope = axial_rope_tables(seq_len)
    for i in range(DEPTH):
        need_ctx = i < DEPTH - 1
        mod_l = modulation(c, w_ada[i], b_ada[i])
        mod_c = modulation(c_ctx[None, :], w_ada[i], b_ada[i])
        sh1_l, sc1_l, g1_l, sh2_l, sc2_l, g2_l = jnp.split(mod_l, N_MOD, axis=-1)
        sh1_c, sc1_c, g1_c, sh2_c, sc2_c, g2_c = jnp.split(mod_c, N_MOD, axis=-1)

        h_l = rms_norm(x, norm_mix_w[i]) * (1.0 + sc1_l) + sh1_l
        h_c = rms_norm(ctx, norm_mix_w[i]) * (1.0 + sc1_c) + sh1_c
        q_l, k_l, v_l, z_l, xbc_l, dtr_l = split_projection(h_l @ w_in[i])
        q_c, k_c, v_c, z_c, xbc_c, dtr_c = split_projection(h_c @ w_in[i])
        att_l, att_c = attention_mixer(q_l, k_l, v_l, q_c, k_c, v_c, rope, q_norm_w[i], k_norm_w[i], need_ctx)
        ssd_l, ssd_c = ssd_mixer(z_l, xbc_l, dtr_l, z_c, xbc_c, dtr_c, ssd_conv_w[i], ssd_conv_b[i],
                                 dt_bias[i], a_log[i], d_skip[i], ssd_norm_w[i], need_ctx)
        x = x + g1_l * (jnp.concatenate([att_l, ssd_l], axis=-1) @ w_out[i])
        h2_l = rms_norm(x, norm_mlp_w[i]) * (1.0 + sc2_l) + sh2_l
        x = x + g2_l * conv_ffn(h2_l, w_up[i], ffn_conv_w[i], ffn_conv_b[i], w_down[i])

        if need_ctx:
            ctx = ctx + g1_c * (jnp.concatenate([att_c, ssd_c], axis=-1) @ w_out[i])
            h2_c = rms_norm(ctx, norm_mlp_w[i]) * (1.0 + sc2_c) + sh2_c
            ctx = ctx + g2_c * conv_ffn(h2_c, w_up[i], ffn_conv_w[i], ffn_conv_b[i], w_down[i])
    return x
```

```python
import functools
import math

import jax
import jax.numpy as jnp
from jax import lax
from jax.experimental import pallas as pl
from jax.experimental.pallas import tpu as pltpu

D_MODEL = 2048
BATCH = 4
SEQ = 2048
DEPTH = 4
GRID_W = 64
CTX_LEN = 256
ATTN_HEADS = 8
ATTN_KV_HEADS = 2
GQA_GROUP = ATTN_HEADS // ATTN_KV_HEADS
HEAD_DIM = 128
ATTN_WIDTH = ATTN_HEADS * HEAD_DIM
KV_WIDTH = ATTN_KV_HEADS * HEAD_DIM
Q_BLOCK = 128
ROPE_THETA = 10000.0
ROPE_PAIRS = HEAD_DIM // 4
SSD_INNER = D_MODEL // 2
SSD_HEAD_DIM = 64
SSD_HEADS = SSD_INNER // SSD_HEAD_DIM
SSD_GROUPS = 2
SSD_HEADS_PER_GROUP = SSD_HEADS // SSD_GROUPS
SSD_STATE = 128
SSD_CONV = 3
SSD_CHUNK = 128
XBC_WIDTH = SSD_INNER + 2 * SSD_GROUPS * SSD_STATE
MIX_WIDTH = ATTN_WIDTH + SSD_INNER
MAIN_COLS = ATTN_WIDTH + 2 * KV_WIDTH + SSD_INNER + XBC_WIDTH
D_FF = 5632
N_MOD = 6
EPS = 1e-6

CTX_ROWS = BATCH * CTX_LEN
LAT_ROWS = BATCH * SEQ
N_ROWS = CTX_ROWS + LAT_ROWS
MOD_ROWS = 8
MOD_CTX_ROW = BATCH

F32 = jnp.float32
BF16 = jnp.bfloat16

V7X_VMEM_BYTES = 64 * 1024 * 1024
VMEM_CAP = V7X_VMEM_BYTES - 8 * 1024 * 1024


def _vmem_limit(nbytes):
    return int(min(VMEM_CAP, max(32 * 1024 * 1024, nbytes)))


def _mod_row(gi, tm):
    n_ctx_tiles = CTX_ROWS // tm
    tiles_per_seq = SEQ // tm
    return jnp.where(gi < n_ctx_tiles, MOD_CTX_ROW, jnp.maximum(gi - n_ctx_tiles, 0) // tiles_per_seq)


def _ada_kernel(c_ref, w_ref, b_ref, o_ref):
    c = c_ref[...]
    s = c * (1.0 / (1.0 + jnp.exp(-c)))
    acc = jnp.dot(s.astype(BF16), w_ref[...].astype(BF16), preferred_element_type=F32)
    o_ref[...] = acc + b_ref[...]


def ada_modulation(cond, w_ada, b_ada, *, tn=1024):
    ncols = N_MOD * D_MODEL
    return pl.pallas_call(
        _ada_kernel,
        out_shape=jax.ShapeDtypeStruct((DEPTH, MOD_ROWS, ncols), F32),
        grid=(DEPTH, ncols // tn),
        in_specs=[
            pl.BlockSpec((MOD_ROWS, D_MODEL), lambda l, j: (0, 0)),
            pl.BlockSpec((None, D_MODEL, tn), lambda l, j: (l, 0, j)),
            pl.BlockSpec((None, 1, tn), lambda l, j: (l, 0, j)),
        ],
        out_specs=pl.BlockSpec((None, MOD_ROWS, tn), lambda l, j: (l, 0, j)),
        compiler_params=pltpu.CompilerParams(
            dimension_semantics=("arbitrary", "arbitrary"),
            vmem_limit_bytes=_vmem_limit(3 * D_MODEL * tn * 4 + (8 << 20))),
        name="ada_modulation",
    )(cond, w_ada, b_ada.reshape(DEPTH, 1, ncols))


def _norm_mod_kernel(x_ref, nw_ref, sh_ref, sc_ref, *rest, with_dt):
    x = x_ref[...]
    y = x * lax.rsqrt(jnp.mean(x * x, axis=-1, keepdims=True) + EPS)
    h = (y * nw_ref[...]) * (1.0 + sc_ref[...]) + sh_ref[...]
    hb = h.astype(BF16)
    if with_dt:
        wdt_ref, h_ref, dt_ref = rest
        dt_ref[...] = jnp.dot(hb, wdt_ref[...].astype(BF16), preferred_element_type=F32)
    else:
        (h_ref,) = rest
    h_ref[...] = hb


def norm_mod(x, norm_w, mod, sh_idx, *, row_off=0, w_dt=None, tm=512):
    nrows = N_ROWS - row_off
    toff = row_off // tm
    with_dt = w_dt is not None
    in_specs = [
        pl.BlockSpec((tm, D_MODEL), lambda i: (i + toff, 0)),
        pl.BlockSpec((1, D_MODEL), lambda i: (0, 0)),
        pl.BlockSpec((None, 1, D_MODEL), lambda i: (_mod_row(i + toff, tm), 0, sh_idx)),
        pl.BlockSpec((None, 1, D_MODEL), lambda i: (_mod_row(i + toff, tm), 0, sh_idx + 1)),
    ]
    args = [x, norm_w.reshape(1, D_MODEL), mod, mod]
    out_shape = [jax.ShapeDtypeStruct((nrows, D_MODEL), BF16)]
    out_specs = [pl.BlockSpec((tm, D_MODEL), lambda i: (i, 0))]
    if with_dt:
        ndt = w_dt.shape[-1]
        in_specs.append(pl.BlockSpec((D_MODEL, ndt), lambda i: (0, 0)))
        args.append(w_dt)
        out_shape.append(jax.ShapeDtypeStruct((nrows, ndt), F32))
        out_specs.append(pl.BlockSpec((tm, ndt), lambda i: (i, 0)))
    res = pl.pallas_call(
        functools.partial(_norm_mod_kernel, with_dt=with_dt),
        out_shape=out_shape,
        grid=(nrows // tm,),
        in_specs=in_specs,
        out_specs=out_specs,
        compiler_params=pltpu.CompilerParams(
            dimension_semantics=("arbitrary",),
            vmem_limit_bytes=_vmem_limit(tm * D_MODEL * (2 * 4 + 2 * 2 + 3 * 4) + (4 << 20))),
        name="norm_mod",
    )(*args)
    return res if with_dt else res[0]


def _mm_kernel(*refs, n_lhs, resid):
    x_refs = refs[:n_lhs]
    w_refs = refs[n_lhs:2 * n_lhs]
    pos = 2 * n_lhs
    if resid:
        res_ref, gate_ref = refs[pos:pos + 2]
        pos += 2
    o_ref = refs[pos]
    wbf_refs = refs[pos + 1:pos + 1 + n_lhs]

    @pl.when(pl.program_id(1) == 0)
    def _():
        for w_ref, wbf_ref in zip(w_refs, wbf_refs):
            wbf_ref[...] = w_ref[...].astype(BF16)

    acc = jnp.dot(x_refs[0][...], wbf_refs[0][...], preferred_element_type=F32)
    for x_ref, wbf_ref in zip(x_refs[1:], wbf_refs[1:]):
        acc = acc + jnp.dot(x_ref[...], wbf_ref[...], preferred_element_type=F32)
    if resid:
        acc = res_ref[...] + gate_ref[...] * acc
    o_ref[...] = acc.astype(o_ref.dtype)


def matmul(xs, w, *, ncols, col_off=0, tm, tn, out_dtype, resid=None, name):
    m = xs[0].shape[0]
    toff = (N_ROWS - m) // tm
    n_lhs = len(xs)
    kdims = [x.shape[1] for x in xs]
    assert all(k == kdims[0] for k in kdims) and sum(kdims) == w.shape[0]
    kb = kdims[0]
    coff = col_off // tn
    in_specs = [pl.BlockSpec((tm, kb), lambda j, i: (i, 0)) for _ in xs]
    in_specs += [pl.BlockSpec((kb, tn), functools.partial(lambda j, i, k: (k, j + coff), k=k)) for k in range(n_lhs)]
    args = list(xs) + [w] * n_lhs
    if resid is not None:
        x_res, mod, gate_idx = resid
        gcol = gate_idx * (D_MODEL // tn)
        in_specs.append(pl.BlockSpec((tm, tn), lambda j, i: (i + toff, j)))
        in_specs.append(pl.BlockSpec((None, 1, tn), lambda j, i: (_mod_row(i + toff, tm), 0, gcol + j)))
        args += [x_res, mod]
    osz = jnp.dtype(out_dtype).itemsize
    est = (2 * n_lhs * tm * kb * 2 + 2 * n_lhs * kb * tn * 4 + n_lhs * kb * tn * 2
           + 2 * tm * tn * osz + 3 * tm * tn * 4 + (2 * tm * tn * 4 if resid is not None else 0))
    return pl.pallas_call(
        functools.partial(_mm_kernel, n_lhs=n_lhs, resid=resid is not None),
        out_shape=jax.ShapeDtypeStruct((m, ncols), out_dtype),
        grid=(ncols // tn, m // tm),
        in_specs=in_specs,
        out_specs=pl.BlockSpec((tm, tn), lambda j, i: (i, j)),
        scratch_shapes=[pltpu.VMEM((kb, tn), BF16) for _ in xs],
        compiler_params=pltpu.CompilerParams(
            dimension_semantics=("arbitrary", "arbitrary"),
            vmem_limit_bytes=_vmem_limit(est + (4 << 20))),
        name=name,
    )(*args)


def _rms_norm(x, w):
    xf = x.astype(F32)
    y = xf * lax.rsqrt(jnp.mean(xf * xf, axis=-1, keepdims=True) + EPS)
    return (y * w.astype(F32)).astype(x.dtype)


def _depthwise_conv(x, w, b):
    k = w.shape[0]
    y = lax.conv_general_dilated(x, w[:, None, :], window_strides=(1,),
                                 padding=((k // 2, k // 2),),
                                 dimension_numbers=('NWC', 'WIO', 'NWC'),
                                 feature_group_count=x.shape[-1])
    return y + b


def _axial_rope_tables(seq_len):
    rows = seq_len // GRID_W
    row, col = jnp.meshgrid(jnp.arange(rows), jnp.arange(GRID_W), indexing='ij')
    inv_freq = ROPE_THETA ** (-jnp.arange(ROPE_PAIRS, dtype=F32) / ROPE_PAIRS)
    ang_r = row.reshape(-1).astype(F32)[:, None] * inv_freq
    ang_c = col.reshape(-1).astype(F32)[:, None] * inv_freq
    return (jnp.cos(ang_r), jnp.sin(ang_r), jnp.cos(ang_c), jnp.sin(ang_c))


def _rope_half(x, cos, sin):
    x1, x2 = jnp.split(x, 2, axis=-1)
    c = cos[:, None, :]
    s = sin[:, None, :]
    return jnp.concatenate([x1 * c - x2 * s, x2 * c + x1 * s], axis=-1)


def _apply_axial_rope(x, tables):
    cos_r, sin_r, cos_c, sin_c = tables
    x_row, x_col = jnp.split(x, 2, axis=-1)
    return jnp.concatenate([_rope_half(x_row, cos_r, sin_r), _rope_half(x_col, cos_c, sin_c)], axis=-1).astype(x.dtype)


def _grouped_sdpa(q, k, v):
    s = jnp.einsum('bqkgd,bskd->bkgqs', q, k).astype(F32) * (HEAD_DIM ** -0.5)
    p = jax.nn.softmax(s, axis=-1).astype(v.dtype)
    return jnp.einsum('bkgqs,bskd->bqkgd', p, v)


def _attention_mixer(q_l, k_l, v_l, q_c, k_c, v_c, rope, q_norm_w, k_norm_w, need_ctx_out):
    bsz, seq_len = q_l.shape[:2]
    ctx_len = q_c.shape[1]
    q_l = _apply_axial_rope(_rms_norm(q_l, q_norm_w), rope)
    k_l = _apply_axial_rope(_rms_norm(k_l, k_norm_w), rope)
    q_c = _rms_norm(q_c, q_norm_w)
    k_c = _rms_norm(k_c, k_norm_w)
    k_all = jnp.concatenate([k_c, k_l], axis=1)
    v_all = jnp.concatenate([v_c, v_l], axis=1)
    n_blk = seq_len // Q_BLOCK
    q_blocks = q_l.reshape(bsz, n_blk, Q_BLOCK, ATTN_KV_HEADS, GQA_GROUP, HEAD_DIM).swapaxes(0, 1)
    o_l = lax.map(lambda qb: _grouped_sdpa(qb, k_all, v_all), q_blocks)
    o_l = o_l.swapaxes(0, 1).reshape(bsz, seq_len, ATTN_WIDTH)
    o_c = None
    if need_ctx_out:
        q_cg = q_c.reshape(bsz, ctx_len, ATTN_KV_HEADS, GQA_GROUP, HEAD_DIM)
        o_c = _grouped_sdpa(q_cg, k_c, v_c).reshape(bsz, ctx_len, ATTN_WIDTH)
    return o_l, o_c


def _ssd_chunked(xs, dt, a, bm, cm, init_state):
    bsz, t = xs.shape[:2]
    nc = t // SSD_CHUNK
    e = SSD_HEADS_PER_GROUP
    xdt = (xs.astype(F32) * dt[..., None]).reshape(bsz, nc, SSD_CHUNK, SSD_GROUPS, e, SSD_HEAD_DIM)
    da = (dt * a).reshape(bsz, nc, SSD_CHUNK, SSD_GROUPS, e)
    bm = bm.astype(F32).reshape(bsz, nc, SSD_CHUNK, SSD_GROUPS, SSD_STATE)
    cm = cm.astype(F32).reshape(bsz, nc, SSD_CHUNK, SSD_GROUPS, SSD_STATE)
    cs = jnp.cumsum(da, axis=2)
    causal = jnp.tril(jnp.ones((SSD_CHUNK, SSD_CHUNK), dtype=bool))[:, :, None, None]
    seg = cs[:, :, :, None] - cs[:, :, None, :]
    decay = jnp.exp(jnp.where(causal, seg, -jnp.inf))
    cb = jnp.einsum('bclgn,bcsgn->bclsg', cm, bm)
    y_diag = jnp.einsum('bclsg,bclsge,bcsgep->bclgep', cb, decay, xdt)
    to_end = jnp.exp(cs[:, :, -1:] - cs)
    chunk_states = jnp.einsum('bclgn,bclge,bclgep->bcgepn', bm, to_end, xdt)
    chunk_decay = jnp.exp(cs[:, :, -1])

    def carry_step(state, inp):
        st, dc = inp
        return state * dc[..., None, None] + st, state

    final_state, prev = lax.scan(carry_step, init_state,
                                 (jnp.moveaxis(chunk_states, 1, 0), jnp.moveaxis(chunk_decay, 1, 0)))
    prev = jnp.moveaxis(prev, 0, 1)
    y_off = jnp.einsum('bclgn,bcgepn,bclge->bclgep', cm, prev, jnp.exp(cs))
    y = (y_diag + y_off).reshape(bsz, t, SSD_HEADS, SSD_HEAD_DIM)
    return y, final_state


def _ssd_mixer(z_l, xbc_l, dtr_l, z_c, xbc_c, dtr_c, conv_w, conv_b, dt_bias, a_log, d_skip, norm_w, need_ctx_out):
    a = -jnp.exp(a_log.astype(F32))

    def prep(xbc, dt_raw):
        xbc = jax.nn.silu(_depthwise_conv(xbc, conv_w, conv_b))
        xs, bm, cm = jnp.split(xbc, [SSD_INNER, SSD_INNER + SSD_GROUPS * SSD_STATE], axis=-1)
        bsz, t = xs.shape[:2]
        xs = xs.reshape(bsz, t, SSD_HEADS, SSD_HEAD_DIM)
        bm = bm.reshape(bsz, t, SSD_GROUPS, SSD_STATE)
        cm = cm.reshape(bsz, t, SSD_GROUPS, SSD_STATE)
        dt = jax.nn.softplus(dt_raw.astype(F32).reshape(bsz, t, 2, SSD_HEADS) + dt_bias.astype(F32))
        return xs, bm, cm, dt

    xs_c, b_c, c_c, dt_c = prep(xbc_c, dtr_c)
    xs_l, b_l, c_l, dt_l = prep(xbc_l, dtr_l)
    bsz = xs_l.shape[0]
    flip = lambda u: jnp.flip(u, axis=1)
    zero = jnp.zeros((bsz, SSD_GROUPS, SSD_HEADS_PER_GROUP, SSD_HEAD_DIM, SSD_STATE), F32)
    y_cf, s_f = _ssd_chunked(xs_c, dt_c[:, :, 0], a[0], b_c, c_c, zero)
    y_lf, _ = _ssd_chunked(xs_l, dt_l[:, :, 0], a[0], b_l, c_l, s_f)
    y_cb, s_b = _ssd_chunked(flip(xs_c), flip(dt_c[:, :, 1]), a[1], flip(b_c), flip(c_c), zero)
    y_lb, _ = _ssd_chunked(flip(xs_l), flip(dt_l[:, :, 1]), a[1], flip(b_l), flip(c_l), s_b)

    def finish(y_f, y_b, xs, z):
        bsz_, t = xs.shape[:2]
        y = y_f + flip(y_b) + d_skip.astype(F32)[:, None] * xs.astype(F32)
        y = y.reshape(bsz_, t, SSD_INNER) * jax.nn.silu(z.astype(F32))
        return _rms_norm(y, norm_w).astype(z.dtype)

    o_l = finish(y_lf, y_lb, xs_l, z_l)
    o_c = finish(y_cf, y_cb, xs_c, z_c) if need_ctx_out else None
    return o_l, o_c


def _split_tokens(p):
    return p[:CTX_ROWS].reshape(BATCH, CTX_LEN, -1), p[CTX_ROWS:].reshape(BATCH, SEQ, -1)


def _join_tokens(p_c, p_l):
    return jnp.concatenate([p_c.reshape(CTX_ROWS, -1), p_l.reshape(LAT_ROWS, -1)], axis=0)


def kernel(x, c, ctx, c_ctx, w_ada, b_ada, norm_mix_w, w_in, q_norm_w, k_norm_w, ssd_conv_w, ssd_conv_b, dt_bias,
           a_log, d_skip, ssd_norm_w, w_out, norm_mlp_w, w_up, ffn_conv_w, ffn_conv_b, w_down):
    rope = _axial_rope_tables(SEQ)
    cond = jnp.zeros((MOD_ROWS, D_MODEL), F32).at[:BATCH].set(c).at[MOD_CTX_ROW].set(c_ctx)
    mod_all = ada_modulation(cond, w_ada, b_ada)
    w_dt_all = w_in[:, :, MAIN_COLS:]
    xt = jnp.concatenate([ctx.reshape(CTX_ROWS, D_MODEL), x.reshape(LAT_ROWS, D_MODEL)], axis=0)

    for l in range(DEPTH):
        need_ctx = l < DEPTH - 1
        mod = mod_all[l].reshape(MOD_ROWS, 1, N_MOD * D_MODEL)
        h, dt_raw = norm_mod(xt, norm_mix_w[l], mod, 0, w_dt=w_dt_all[l])
        p = matmul([h], w_in[l], ncols=MAIN_COLS, tm=1024, tn=512, out_dtype=F32, name="in_proj")
        p_c, p_l = _split_tokens(p)
        dt_c, dt_l = _split_tokens(dt_raw)
        cuts = [ATTN_WIDTH, ATTN_WIDTH + KV_WIDTH, ATTN_WIDTH + 2 * KV_WIDTH, ATTN_WIDTH + 2 * KV_WIDTH + SSD_INNER]

        def split(pp):
            q, k, v, z, xbc = jnp.split(pp, cuts, axis=-1)
            t = pp.shape[1]
            return (q.reshape(BATCH, t, ATTN_HEADS, HEAD_DIM), k.reshape(BATCH, t, ATTN_KV_HEADS, HEAD_DIM),
                    v.reshape(BATCH, t, ATTN_KV_HEADS, HEAD_DIM), z, xbc)

        q_l, k_l, v_l, z_l, xbc_l = split(p_l)
        q_c, k_c, v_c, z_c, xbc_c = split(p_c)
        att_l, att_c = _attention_mixer(q_l, k_l, v_l, q_c, k_c, v_c, rope, q_norm_w[l], k_norm_w[l], need_ctx)
        ssd_l, ssd_c = _ssd_mixer(z_l, xbc_l, dt_l, z_c, xbc_c, dt_c, ssd_conv_w[l], ssd_conv_b[l],
                                  dt_bias[l], a_log[l], d_skip[l], ssd_norm_w[l], need_ctx)
        if need_ctx:
            att = _join_tokens(att_c, att_l).astype(BF16)
            ssd = _join_tokens(ssd_c, ssd_l).astype(BF16)
        else:
            att = att_l.reshape(LAT_ROWS, ATTN_WIDTH).astype(BF16)
            ssd = ssd_l.reshape(LAT_ROWS, SSD_INNER).astype(BF16)
        row_off = 0 if need_ctx else CTX_ROWS
        xt2 = matmul([att, ssd], w_out[l], ncols=D_MODEL, tm=1024, tn=512, out_dtype=F32,
                     resid=(xt, mod, 2), name="out_proj")
        if not need_ctx:
            xt2 = jnp.concatenate([xt[:CTX_ROWS], xt2], axis=0)
        h2 = norm_mod(xt2, norm_mlp_w[l], mod, 3, row_off=row_off)
        u = matmul([h2], w_up[l], ncols=2 * D_FF, tm=1024, tn=512, out_dtype=F32, name="ffn_up")
        if need_ctx:
            u_c, u_l = _split_tokens(u)
            u = _join_tokens(_depthwise_conv(u_c, ffn_conv_w[l], ffn_conv_b[l]),
                             _depthwise_conv(u_l, ffn_conv_w[l], ffn_conv_b[l]))
        else:
            u = _depthwise_conv(u.reshape(BATCH, SEQ, -1), ffn_conv_w[l], ffn_conv_b[l]).reshape(LAT_ROWS, -1)
        gate, val = jnp.split(u, 2, axis=-1)
        g = (jax.nn.silu(gate) * val).astype(BF16)
        xt3 = matmul([g], w_down[l], ncols=D_MODEL, tm=512, tn=512, out_dtype=F32,
                     resid=(xt2, mod, 5), name="ffn_down")
        if not need_ctx:
            xt3 = jnp.concatenate([xt2[:CTX_ROWS], xt3], axis=0)
        xt = xt3
    return xt[CTX_ROWS:].reshape(BATCH, SEQ, D_MODEL)
```

```python
import functools

import jax
import jax.numpy as jnp
from jax import lax
from jax.experimental import pallas as pl
from jax.experimental.pallas import tpu as pltpu

D_MODEL = 2048
BATCH = 4
SEQ = 2048
DEPTH = 4
GRID_W = 64
CTX_LEN = 256
ATTN_HEADS = 8
ATTN_KV_HEADS = 2
GQA_GROUP = ATTN_HEADS // ATTN_KV_HEADS
HEAD_DIM = 128
ATTN_WIDTH = ATTN_HEADS * HEAD_DIM
KV_WIDTH = ATTN_KV_HEADS * HEAD_DIM
ROPE_THETA = 10000.0
ROPE_PAIRS = HEAD_DIM // 4
SSD_INNER = D_MODEL // 2
SSD_HEAD_DIM = 64
SSD_HEADS = SSD_INNER // SSD_HEAD_DIM
SSD_GROUPS = 2
SSD_STATE = 128
SSD_CHUNK = 128
XBC_WIDTH = SSD_INNER + 2 * SSD_GROUPS * SSD_STATE
QKVZ_WIDTH = ATTN_WIDTH + 2 * KV_WIDTH + SSD_INNER
MAIN_COLS = QKVZ_WIDTH + XBC_WIDTH
D_FF = 5632
N_MOD = 6
EPS = 1e-6

LAT_ROWS = BATCH * SEQ
CTX_ROWS = BATCH * CTX_LEN
N_ROWS = LAT_ROWS + CTX_ROWS
MOD_ROWS = 8
MOD_CTX_ROW = BATCH
HALO = 16

F32 = jnp.float32
BF16 = jnp.bfloat16

V7X_VMEM_BYTES = 64 * 1024 * 1024
VMEM_CAP = V7X_VMEM_BYTES - 8 * 1024 * 1024


def _vmem_limit(nbytes):
    return int(min(VMEM_CAP, max(32 * 1024 * 1024, nbytes)))


def _mod_row(gi, tm):
    return jnp.where(gi < LAT_ROWS // tm, gi // (SEQ // tm), MOD_CTX_ROW)


def _silu(v):
    return v * (1.0 / (1.0 + jnp.exp(-v)))


def _ada_kernel(c_ref, w_ref, b_ref, o_ref):
    s = _silu(c_ref[...])
    acc = jnp.dot(s.astype(BF16), w_ref[...].astype(BF16), preferred_element_type=F32)
    o_ref[...] = acc + b_ref[...]


def ada_modulation(cond, w_ada, b_ada, *, tn=1024):
    ncols = N_MOD * D_MODEL
    return pl.pallas_call(
        _ada_kernel,
        out_shape=jax.ShapeDtypeStruct((DEPTH, MOD_ROWS, ncols), F32),
        grid=(DEPTH, ncols // tn),
        in_specs=[
            pl.BlockSpec((MOD_ROWS, D_MODEL), lambda l, j: (0, 0)),
            pl.BlockSpec((None, D_MODEL, tn), lambda l, j: (l, 0, j)),
            pl.BlockSpec((None, 1, tn), lambda l, j: (l, 0, j)),
        ],
        out_specs=pl.BlockSpec((None, MOD_ROWS, tn), lambda l, j: (l, 0, j)),
        compiler_params=pltpu.CompilerParams(
            dimension_semantics=("arbitrary", "arbitrary"),
            vmem_limit_bytes=_vmem_limit(3 * D_MODEL * tn * 4 + (8 << 20))),
        name="ada_modulation",
    )(cond, w_ada, b_ada.reshape(DEPTH, 1, ncols))


def _norm_mod_kernel(x_ref, nw_ref, sh_ref, sc_ref, *rest, with_dt):
    x = x_ref[...]
    y = x * lax.rsqrt(jnp.mean(x * x, axis=-1, keepdims=True) + EPS)
    h = (y * nw_ref[...]) * (1.0 + sc_ref[...]) + sh_ref[...]
    hb = h.astype(BF16)
    if with_dt:
        wdt_ref, wdtt_ref, h_ref, dt_ref, dtt_ref = rest
        dt = jnp.dot(hb, wdt_ref[...].astype(BF16), preferred_element_type=F32)
        dtt = lax.dot_general(wdtt_ref[...].astype(BF16), hb, (((1,), (1,)), ((), ())),
                              preferred_element_type=F32)
        for d in range(2):
            dt_ref[d] = dt[:, d * SSD_HEADS:(d + 1) * SSD_HEADS]
            dtt_ref[d] = dtt[d * SSD_HEADS:(d + 1) * SSD_HEADS, :]
    else:
        (h_ref,) = rest
    h_ref[...] = hb


def norm_mod(x, nrows, norm_w, mod, sh_idx, *, w_dt=None, tm=512):
    with_dt = w_dt is not None
    in_specs = [
        pl.BlockSpec((tm, D_MODEL), lambda i: (i, 0)),
        pl.BlockSpec((1, D_MODEL), lambda i: (0, 0)),
        pl.BlockSpec((None, 1, D_MODEL), lambda i: (_mod_row(i, tm), 0, sh_idx)),
        pl.BlockSpec((None, 1, D_MODEL), lambda i: (_mod_row(i, tm), 0, sh_idx + 1)),
    ]
    args = [x, norm_w.reshape(1, D_MODEL), mod, mod]
    out_shape = [jax.ShapeDtypeStruct((nrows, D_MODEL), BF16)]
    out_specs = [pl.BlockSpec((tm, D_MODEL), lambda i: (i, 0))]
    if with_dt:
        in_specs += [pl.BlockSpec((D_MODEL, 2 * SSD_HEADS), lambda i: (0, 0)),
                     pl.BlockSpec((2 * SSD_HEADS, D_MODEL), lambda i: (0, 0))]
        args += [w_dt, w_dt.T]
        out_shape += [jax.ShapeDtypeStruct((2, nrows, SSD_HEADS), F32),
                      jax.ShapeDtypeStruct((2, SSD_HEADS, nrows), F32)]
        out_specs += [pl.BlockSpec((2, tm, SSD_HEADS), lambda i: (0, i, 0)),
                      pl.BlockSpec((2, SSD_HEADS, tm), lambda i: (0, 0, i))]
    res = pl.pallas_call(
        functools.partial(_norm_mod_kernel, with_dt=with_dt),
        out_shape=out_shape,
        grid=(nrows // tm,),
        in_specs=in_specs,
        out_specs=out_specs,
        compiler_params=pltpu.CompilerParams(
            dimension_semantics=("arbitrary",),
            vmem_limit_bytes=_vmem_limit(tm * D_MODEL * (2 * 4 + 2 * 2 + 3 * 4) + (4 << 20))),
        name="norm_mod",
    )(*args)
    return res if with_dt else res[0]


def _mm_kernel(*refs, n_lhs, resid):
    x_refs = refs[:n_lhs]
    w_refs = refs[n_lhs:2 * n_lhs]
    pos = 2 * n_lhs
    if resid:
        res_ref, gate_ref = refs[pos:pos + 2]
        pos += 2
    o_ref = refs[pos]
    wbf_refs = refs[pos + 1:pos + 1 + n_lhs]

    @pl.when(pl.program_id(1) == 0)
    def _():
        for w_ref, wbf_ref in zip(w_refs, wbf_refs):
            wbf_ref[...] = w_ref[...].astype(BF16)

    acc = jnp.dot(x_refs[0][...], wbf_refs[0][...], preferred_element_type=F32)
    for x_ref, wbf_ref in zip(x_refs[1:], wbf_refs[1:]):
        acc = acc + jnp.dot(x_ref[...], wbf_ref[...], preferred_element_type=F32)
    if resid:
        acc = res_ref[...] + gate_ref[...] * acc
    o_ref[...] = acc.astype(o_ref.dtype)


def matmul(xs, w, *, m, ncols, tm, tn, out_dtype, out_col_map=None, resid=None, name):
    n_lhs = len(xs)
    kb = xs[0].shape[1]
    assert all(x.shape[0] >= m and x.shape[1] == kb for x in xs) and n_lhs * kb == w.shape[0]
    ocm = out_col_map if out_col_map is not None else (lambda j: j)
    in_specs = [pl.BlockSpec((tm, kb), lambda j, i: (i, 0)) for _ in xs]
    in_specs += [pl.BlockSpec((kb, tn), functools.partial(lambda j, i, k: (k, j), k=k)) for k in range(n_lhs)]
    args = list(xs) + [w] * n_lhs
    if resid is not None:
        x_res, mod, gate_idx = resid
        gcol = gate_idx * (D_MODEL // tn)
        in_specs.append(pl.BlockSpec((tm, tn), lambda j, i: (i, j)))
        in_specs.append(pl.BlockSpec((None, 1, tn), lambda j, i: (_mod_row(i, tm), 0, gcol + j)))
        args += [x_res, mod]
    osz = jnp.dtype(out_dtype).itemsize
    est = (2 * n_lhs * tm * kb * 2 + 2 * n_lhs * kb * tn * 4 + n_lhs * kb * tn * 2
           + 2 * tm * tn * osz + 3 * tm * tn * 4 + (2 * tm * tn * 4 if resid is not None else 0))
    return pl.pallas_call(
        functools.partial(_mm_kernel, n_lhs=n_lhs, resid=resid is not None),
        out_shape=jax.ShapeDtypeStruct((m, ncols), out_dtype),
        grid=(ncols // tn, m // tm),
        in_specs=in_specs,
        out_specs=pl.BlockSpec((tm, tn), lambda j, i: (i, ocm(j))),
        scratch_shapes=[pltpu.VMEM((kb, tn), BF16) for _ in xs],
        compiler_params=pltpu.CompilerParams(
            dimension_semantics=("arbitrary", "arbitrary"),
            vmem_limit_bytes=_vmem_limit(est + (4 << 20))),
        name=name,
    )(*args)


def _seq_edges(r):
    in_ctx = r >= LAT_ROWS
    first = ((r & (SEQ - 1)) == 0) | (in_ctx & ((r & (CTX_LEN - 1)) == 0))
    last = (((r + 1) & (SEQ - 1)) == 0) | (in_ctx & (((r + 1) & (CTX_LEN - 1)) == 0))
    return first, last


def _mm_conv_kernel(x_ref, xp_ref, xn_ref, *refs, n_w, tm):
    w_refs = refs[:n_w]
    cw_refs = refs[n_w:2 * n_w]
    cb_refs = refs[2 * n_w:3 * n_w]
    o_ref = refs[3 * n_w]
    lhs_ref = refs[3 * n_w + 1]
    wbf_refs = refs[3 * n_w + 2:4 * n_w + 2]
    u_refs = refs[4 * n_w + 2:5 * n_w + 2]
    i = pl.program_id(1)

    @pl.when(i == 0)
    def _():
        for w_ref, wbf_ref in zip(w_refs, wbf_refs):
            wbf_ref[...] = w_ref[...].astype(BF16)

    lhs_ref[0:HALO, :] = xp_ref[...]
    lhs_ref[HALO:HALO + tm, :] = x_ref[...]
    lhs_ref[HALO + tm:, :] = xn_ref[...]
    r = i * tm + lax.broadcasted_iota(jnp.int32, (tm, 1), 0)
    first, last = _seq_edges(r)
    ys = []
    for wbf_ref, cw_ref, cb_ref, u_ref in zip(wbf_refs, cw_refs, cb_refs, u_refs):
        u_ref[...] = jnp.dot(lhs_ref[...], wbf_ref[...], preferred_element_type=F32)
        up = jnp.where(first, 0.0, u_ref[HALO - 1:HALO - 1 + tm, :])
        uc = u_ref[HALO:HALO + tm, :]
        un = jnp.where(last, 0.0, u_ref[HALO + 1:HALO + 1 + tm, :])
        ys.append(up * cw_ref[0:1, :] + uc * cw_ref[1:2, :] + un * cw_ref[2:3, :] + cb_ref[...])
    out = _silu(ys[0])
    if n_w == 2:
        out = out * ys[1]
    o_ref[...] = out.astype(o_ref.dtype)


def matmul_conv(x, w, conv_w, conv_b, *, col_offs, conv_col_offs, ncols, tm, tn, name):
    m, kd = x.shape
    n_w = len(col_offs)
    coffs = [c // tn for c in col_offs]
    ccoffs = [c // tn for c in conv_col_offs]
    nhb = m // HALO
    in_specs = [
        pl.BlockSpec((tm, kd), lambda j, i: (i, 0)),
        pl.BlockSpec((HALO, kd), lambda j, i: (jnp.maximum(i * (tm // HALO) - 1, 0), 0)),
        pl.BlockSpec((HALO, kd), lambda j, i: (jnp.minimum((i + 1) * (tm // HALO), nhb - 1), 0)),
    ]
    in_specs += [pl.BlockSpec((kd, tn), functools.partial(lambda j, i, c: (0, j + c), c=c)) for c in coffs]
    in_specs += [pl.BlockSpec((3, tn), functools.partial(lambda j, i, c: (0, j + c), c=c)) for c in ccoffs]
    in_specs += [pl.BlockSpec((1, tn), functools.partial(lambda j, i, c: (0, j + c), c=c)) for c in ccoffs]
    args = [x, x, x] + [w] * n_w + [conv_w] * n_w + [conv_b.reshape(1, -1)] * n_w
    tmh = tm + 2 * HALO
    est = (2 * tm * kd * 2 + tmh * kd * 2 + n_w * (2 * kd * tn * 4 + kd * tn * 2 + tmh * tn * 4)
           + 2 * tm * tn * 2 + 6 * tm * tn * 4)
    return pl.pallas_call(
        functools.partial(_mm_conv_kernel, n_w=n_w, tm=tm),
        out_shape=jax.ShapeDtypeStruct((m, ncols), BF16),
        grid=(ncols // tn, m // tm),
        in_specs=in_specs,
        out_specs=pl.BlockSpec((tm, tn), lambda j, i: (i, j)),
        scratch_shapes=([pltpu.VMEM((tmh, kd), BF16)] + [pltpu.VMEM((kd, tn), BF16)] * n_w
                        + [pltpu.VMEM((tmh, tn), F32)] * n_w),
        compiler_params=pltpu.CompilerParams(
            dimension_semantics=("arbitrary", "arbitrary"),
            vmem_limit_bytes=_vmem_limit(est + (4 << 20))),
        name=name,
    )(*args)


def _rope_tables():
    t = jnp.arange(SEQ)
    inv_freq = ROPE_THETA ** (-jnp.arange(ROPE_PAIRS, dtype=F32) / ROPE_PAIRS)
    ang_r = (t // GRID_W).astype(F32)[:, None] * inv_freq
    ang_c = (t % GRID_W).astype(F32)[:, None] * inv_freq
    cos = jnp.concatenate([jnp.cos(ang_r)] * 2 + [jnp.cos(ang_c)] * 2, axis=-1)
    sin = jnp.concatenate([-jnp.sin(ang_r), jnp.sin(ang_r), -jnp.sin(ang_c), jnp.sin(ang_c)], axis=-1)
    return cos, sin


def _swap_halves(x):
    lane = lax.broadcasted_iota(jnp.int32, x.shape, 1)
    return jnp.where((lane & 63) < 32, pltpu.roll(x, 96, axis=1), pltpu.roll(x, 32, axis=1))


def _head_norm(x, w):
    return x * lax.rsqrt(jnp.mean(x * x, axis=-1, keepdims=True) + EPS) * w


def _attn_kernel(q_ref, kl_ref, kc_ref, vl_ref, vc_ref, cos_ref, sin_ref, qw_ref, kw_ref, o_ref,
                 kall_ref, vall_ref, *, tq, n_lat):
    qi = pl.program_id(2)

    @pl.when(qi == 0)
    def _():
        kl = _head_norm(kl_ref[...].astype(F32), kw_ref[...])
        kl = kl * cos_ref[...] + _swap_halves(kl) * sin_ref[...]
        kall_ref[0:SEQ, :] = kl.astype(BF16)
        kall_ref[SEQ:, :] = _head_norm(kc_ref[...].astype(F32), kw_ref[...]).astype(BF16)
        vall_ref[0:SEQ, :] = vl_ref[...]
        vall_ref[SEQ:, :] = vc_ref[...]

    def attend(latent):
        if latent:
            row0 = pl.multiple_of(qi * tq, tq)
            cos = cos_ref[pl.ds(row0, tq), :]
            sin = sin_ref[pl.ds(row0, tq), :]
            keys, vals = kall_ref[...], vall_ref[...]
        else:
            keys, vals = kall_ref[SEQ:, :], vall_ref[SEQ:, :]
        for h in range(GQA_GROUP):
            qh = _head_norm(q_ref[:, h * HEAD_DIM:(h + 1) * HEAD_DIM].astype(F32), qw_ref[...])
            if latent:
                qh = qh * cos + _swap_halves(qh) * sin
            qh = (qh * (HEAD_DIM ** -0.5)).astype(BF16)
            s = lax.dot_general(qh, keys, (((1,), (1,)), ((), ())), preferred_element_type=F32)
            p = jnp.exp(s - jnp.max(s, axis=-1, keepdims=True))
            den = jnp.sum(p, axis=-1, keepdims=True)
            o = jnp.dot(p.astype(BF16), vals, preferred_element_type=F32) * (1.0 / den)
            o_ref[:, h * HEAD_DIM:(h + 1) * HEAD_DIM] = o.astype(o_ref.dtype)

    @pl.when(qi < n_lat)
    def _():
        attend(True)

    @pl.when(qi >= n_lat)
    def _():
        attend(False)


def attention(qkvz, q_norm_w, k_norm_w, cos, sin, *, with_ctx, tq=256):
    n_lat = SEQ // tq
    n_q = n_lat + (CTX_LEN // tq if with_ctx else 0)
    nrows = N_ROWS if with_ctx else LAT_ROWS
    qw_cols = GQA_GROUP * HEAD_DIM
    k_col0 = (ATTN_WIDTH + SSD_INNER) // HEAD_DIM
    v_col0 = k_col0 + ATTN_KV_HEADS

    def q_row(b, qi):
        return jnp.where(qi < n_lat, b * n_lat + qi, LAT_ROWS // tq + b * (CTX_LEN // tq) + qi - n_lat)

    in_specs = [
        pl.BlockSpec((tq, qw_cols), lambda b, g, qi: (q_row(b, qi), g)),
        pl.BlockSpec((SEQ, HEAD_DIM), lambda b, g, qi: (b, k_col0 + g)),
        pl.BlockSpec((CTX_LEN, HEAD_DIM), lambda b, g, qi: (LAT_ROWS // CTX_LEN + b, k_col0 + g)),
        pl.BlockSpec((SEQ, HEAD_DIM), lambda b, g, qi: (b, v_col0 + g)),
        pl.BlockSpec((CTX_LEN, HEAD_DIM), lambda b, g, qi: (LAT_ROWS // CTX_LEN + b, v_col0 + g)),
        pl.BlockSpec((SEQ, HEAD_DIM), lambda b, g, qi: (0, 0)),
        pl.BlockSpec((SEQ, HEAD_DIM), lambda b, g, qi: (0, 0)),
        pl.BlockSpec((1, HEAD_DIM), lambda b, g, qi: (0, 0)),
        pl.BlockSpec((1, HEAD_DIM), lambda b, g, qi: (0, 0)),
    ]
    nk = SEQ + CTX_LEN
    est = 4 * tq * nk * 4 + 4 * (SEQ * HEAD_DIM * 4) + 8 * nk * HEAD_DIM * 2 + 4 * tq * qw_cols * 2
    return pl.pallas_call(
        functools.partial(_attn_kernel, tq=tq, n_lat=n_lat),
        out_shape=jax.ShapeDtypeStruct((nrows, ATTN_WIDTH), BF16),
        grid=(BATCH, ATTN_KV_HEADS, n_q),
        in_specs=in_specs,
        out_specs=pl.BlockSpec((tq, qw_cols), lambda b, g, qi: (q_row(b, qi), g)),
        scratch_shapes=[pltpu.VMEM((nk, HEAD_DIM), BF16), pltpu.VMEM((nk, HEAD_DIM), BF16)],
        compiler_params=pltpu.CompilerParams(
            dimension_semantics=("arbitrary", "arbitrary", "arbitrary"),
            vmem_limit_bytes=_vmem_limit(est + (8 << 20))),
        name="attention",
    )(qkvz, qkvz, qkvz, qkvz, qkvz, cos, sin, q_norm_w.reshape(1, HEAD_DIM), k_norm_w.reshape(1, HEAD_DIM))


N_CHUNK_STEPS = (CTX_LEN + SEQ) // SSD_CHUNK
CTX_CHUNKS = CTX_LEN // SSD_CHUNK
HEAD_PAIRS = SSD_HEADS // 2
PAIRS_PER_GROUP = HEAD_PAIRS // SSD_GROUPS


def _softplus(v):
    return jnp.maximum(v, 0.0) + jnp.log(1.0 + jnp.exp(-jnp.abs(v)))


def _chunk_row_block(b, d, s):
    lat_chunks = SEQ // SSD_CHUNK
    c_ctx = jnp.where(d == 0, s, CTX_CHUNKS - 1 - s)
    c_lat = jnp.where(d == 0, s - CTX_CHUNKS, N_CHUNK_STEPS - 1 - s)
    return jnp.where(s < CTX_CHUNKS, LAT_ROWS // SSD_CHUNK + b * CTX_CHUNKS + c_ctx, b * lat_chunks + c_lat)


def _ssd_kernel(x_ref, bc_ref, z_ref, dt_ref, dtt_ref, dtb_ref, dtbt_ref, alog_ref, alogt_ref, dskip_ref, nw_ref,
                o_ref, state_ref, yf_ref, y_ref):
    d = pl.program_id(1)
    s = pl.program_id(2)
    L = SSD_CHUNK

    @pl.when(s == 0)
    def _():
        state_ref[...] = jnp.zeros_like(state_ref)

    dt = _softplus(dt_ref[...] + dtb_ref[...])
    dtt = _softplus(dtt_ref[...] + dtbt_ref[...])
    da = dt * (-jnp.exp(alog_ref[...]))
    dat = dtt * (-jnp.exp(alogt_ref[...]))
    row = lax.broadcasted_iota(jnp.int32, (L, L), 0)
    col = lax.broadcasted_iota(jnp.int32, (L, L), 1)
    visited = (col - row) * (1 - 2 * d) <= 0
    vis_f = visited.astype(F32)
    cs = jnp.dot(vis_f, da, preferred_element_type=F32, precision=lax.Precision.HIGHEST)
    cst = lax.dot_general(dat, vis_f, (((1,), (1,)), ((), ())), preferred_element_type=F32,
                          precision=lax.Precision.HIGHEST)
    tot = jnp.sum(da, axis=0, keepdims=True)
    tott = jnp.sum(dat, axis=1, keepdims=True)
    w_end = jnp.exp(tot - cs) * dt
    e_in = jnp.exp(cs)
    cdt = jnp.exp(tott)

    lane = lax.broadcasted_iota(jnp.int32, (L, 2 * SSD_HEAD_DIM), 1)
    srow = lax.broadcasted_iota(jnp.int32, (2 * SSD_HEAD_DIM, SSD_STATE), 0)
    for g in range(SSD_GROUPS):
        bm = bc_ref[:, g * SSD_STATE:(g + 1) * SSD_STATE]
        cm = bc_ref[:, (SSD_GROUPS + g) * SSD_STATE:(SSD_GROUPS + g + 1) * SSD_STATE]
        cb = lax.dot_general(cm, bm, (((1,), (1,)), ((), ())), preferred_element_type=F32)
        bm32 = bm.astype(F32)
        cm32 = cm.astype(F32)
        for kp in range(PAIRS_PER_GROUP):
            k = g * PAIRS_PER_GROUP + kp
            m_parts, ce_parts, bw_parts = [], [], []
            for hh in (2 * k, 2 * k + 1):
                seg = cs[:, hh:hh + 1] - cst[hh:hh + 1, :]
                dec = jnp.exp(jnp.where(visited, seg, -jnp.inf))
                m_parts.append((cb * dec * dtt[hh:hh + 1, :]).astype(BF16))
                ce_parts.append((cm32 * e_in[:, hh:hh + 1]).astype(BF16))
                bw_parts.append((bm32 * w_end[:, hh:hh + 1]).astype(BF16))
            xp = x_ref[:, k * 128:(k + 1) * 128]
            xstack = jnp.concatenate([jnp.where(lane < SSD_HEAD_DIM, xp, jnp.zeros_like(xp)),
                                      jnp.where(lane >= SSD_HEAD_DIM, xp, jnp.zeros_like(xp))], axis=0)
            sp = state_ref[k]
            sstack = jnp.concatenate([jnp.where(srow < SSD_HEAD_DIM, sp, 0.0).astype(BF16),
                                      jnp.where(srow >= SSD_HEAD_DIM, sp, 0.0).astype(BF16)], axis=1)
            y_pair = jnp.dot(jnp.concatenate(m_parts, axis=1), xstack, preferred_element_type=F32)
            y_pair = y_pair + lax.dot_general(jnp.concatenate(ce_parts, axis=1), sstack,
                                              (((1,), (1,)), ((), ())), preferred_element_type=F32)
            s_chunk = lax.dot_general(xstack, jnp.concatenate(bw_parts, axis=0),
                                      (((0,), (0,)), ((), ())), preferred_element_type=F32)
            cd_rows = jnp.where(srow < SSD_HEAD_DIM, cdt[2 * k:2 * k + 1, :], cdt[2 * k + 1:2 * k + 2, :])
            state_ref[k] = sp * cd_rows + s_chunk
            y_ref[:, k * 128:(k + 1) * 128] = y_pair

    @pl.when(d == 0)
    def _():
        yf_ref[pl.ds(pl.multiple_of(s * L, L), L), :] = y_ref[...]

    @pl.when(d == 1)
    def _():
        s_fwd = jnp.where(s < CTX_CHUNKS, CTX_CHUNKS - 1 - s, N_CHUNK_STEPS + CTX_CHUNKS - 1 - s)
        y = (yf_ref[pl.ds(pl.multiple_of(s_fwd * L, L), L), :] + y_ref[...]
             + dskip_ref[...] * x_ref[...].astype(F32))
        y = y * _silu(z_ref[...].astype(F32))
        y = y * lax.rsqrt(jnp.mean(y * y, axis=-1, keepdims=True) + EPS) * nw_ref[...]
        o_ref[...] = y.astype(o_ref.dtype)


def ssd_mixer(xbc, qkvz, dt_raw, dtt_raw, dt_bias, a_log, d_skip, norm_w):
    L = SSD_CHUNK
    rb = _chunk_row_block
    out_rb = lambda b, d, s: rb(b, 1, jnp.where(d == 0, 0, s))
    in_specs = [
        pl.BlockSpec((L, SSD_INNER), lambda b, d, s: (rb(b, d, s), 0)),
        pl.BlockSpec((L, 2 * SSD_GROUPS * SSD_STATE), lambda b, d, s: (rb(b, d, s), SSD_INNER // (2 * SSD_GROUPS * SSD_STATE))),
        pl.BlockSpec((L, SSD_INNER), lambda b, d, s: (rb(b, d, s), ATTN_WIDTH // SSD_INNER)),
        pl.BlockSpec((None, L, SSD_HEADS), lambda b, d, s: (d, rb(b, d, s), 0)),
        pl.BlockSpec((None, SSD_HEADS, L), lambda b, d, s: (d, 0, rb(b, d, s))),
        pl.BlockSpec((None, 1, SSD_HEADS), lambda b, d, s: (d, 0, 0)),
        pl.BlockSpec((None, SSD_HEADS, 1), lambda b, d, s: (d, 0, 0)),
        pl.BlockSpec((None, 1, SSD_HEADS), lambda b, d, s: (d, 0, 0)),
        pl.BlockSpec((None, SSD_HEADS, 1), lambda b, d, s: (d, 0, 0)),
        pl.BlockSpec((1, SSD_INNER), lambda b, d, s: (0, 0)),
        pl.BlockSpec((1, SSD_INNER), lambda b, d, s: (0, 0)),
    ]
    args = [xbc, xbc, qkvz, dt_raw, dtt_raw,
            dt_bias.reshape(2, 1, SSD_HEADS), dt_bias.reshape(2, SSD_HEADS, 1),
            a_log.reshape(2, 1, SSD_HEADS), a_log.reshape(2, SSD_HEADS, 1),
            jnp.repeat(d_skip, SSD_HEAD_DIM).reshape(1, SSD_INNER), norm_w.reshape(1, SSD_INNER)]
    est = (CTX_LEN + SEQ) * SSD_INNER * 4 + HEAD_PAIRS * 128 * 128 * 4 + 16 * L * SSD_INNER * 4
    return pl.pallas_call(
        _ssd_kernel,
        out_shape=jax.ShapeDtypeStruct((N_ROWS, SSD_INNER), BF16),
        grid=(BATCH, 2, N_CHUNK_STEPS),
        in_specs=in_specs,
        out_specs=pl.BlockSpec((L, SSD_INNER), lambda b, d, s: (out_rb(b, d, s), 0)),
        scratch_shapes=[pltpu.VMEM((HEAD_PAIRS, 2 * SSD_HEAD_DIM, SSD_STATE), F32),
                        pltpu.VMEM((CTX_LEN + SEQ, SSD_INNER), F32),
                        pltpu.VMEM((L, SSD_INNER), F32)],
        compiler_params=pltpu.CompilerParams(
            dimension_semantics=("arbitrary", "arbitrary", "arbitrary"),
            vmem_limit_bytes=_vmem_limit(est + (8 << 20))),
        name="ssd_mixer",
    )(*args)


def _qkvz_out_tile(j):
    return jnp.where(j == 2, 4, jnp.where(j > 2, j - 1, j))


def kernel(x, c, ctx, c_ctx, w_ada, b_ada, norm_mix_w, w_in, q_norm_w, k_norm_w, ssd_conv_w, ssd_conv_b, dt_bias,
           a_log, d_skip, ssd_norm_w, w_out, norm_mlp_w, w_up, ffn_conv_w, ffn_conv_b, w_down):
    cos, sin = _rope_tables()
    cond = jnp.zeros((MOD_ROWS, D_MODEL), F32).at[:BATCH].set(c).at[MOD_CTX_ROW].set(c_ctx)
    mod_all = ada_modulation(cond, w_ada, b_ada)
    w_dt_all = w_in[:, :, MAIN_COLS:]
    xt = jnp.concatenate([x.reshape(LAT_ROWS, D_MODEL), ctx.reshape(CTX_ROWS, D_MODEL)], axis=0)

    for l in range(DEPTH):
        need_ctx = l < DEPTH - 1
        nrows = N_ROWS if need_ctx else LAT_ROWS
        mod = mod_all[l].reshape(MOD_ROWS, 1, N_MOD * D_MODEL)
        h, dt_raw, dtt_raw = norm_mod(xt, N_ROWS, norm_mix_w[l], mod, 0, w_dt=w_dt_all[l])
        qkvz = matmul([h], w_in[l], m=N_ROWS, ncols=QKVZ_WIDTH, tm=1024, tn=512, out_dtype=BF16,
                      out_col_map=_qkvz_out_tile, name="in_proj_qkvz")
        xbc = matmul_conv(h, w_in[l], ssd_conv_w[l], ssd_conv_b[l], col_offs=[QKVZ_WIDTH], conv_col_offs=[0],
                          ncols=XBC_WIDTH, tm=512, tn=512, name="in_proj_xbc")
        att = attention(qkvz, q_norm_w[l], k_norm_w[l], cos, sin, with_ctx=need_ctx)
        ssd = ssd_mixer(xbc, qkvz, dt_raw, dtt_raw, dt_bias[l], a_log[l], d_skip[l], ssd_norm_w[l])
        xt = matmul([att, ssd], w_out[l], m=nrows, ncols=D_MODEL, tm=1024, tn=512, out_dtype=F32,
                    resid=(xt, mod, 2), name="out_proj")
        h2 = norm_mod(xt, nrows, norm_mlp_w[l], mod, 3)
        g = matmul_conv(h2, w_up[l], ffn_conv_w[l], ffn_conv_b[l], col_offs=[0, D_FF], conv_col_offs=[0, D_FF],
                        ncols=D_FF, tm=512, tn=512, name="ffn_up")
        xt = matmul([g], w_down[l], m=nrows, ncols=D_MODEL, tm=512, tn=512, out_dtype=F32,
                    resid=(xt, mod, 5), name="ffn_down")
    return xt.reshape(BATCH, SEQ, D_MODEL)
```
